```python
import math
import jax, jax.numpy as jnp
from jax import lax
import numpy as np

D_MODEL = 1024
BATCH = 16
SEQ = 2048
DEPTH = 4

CHUNK = 64
Q_BLOCK = 128
N_A = DEPTH // 2
N_B = DEPTH - N_A
EXPAND_A = 2
E_A = EXPAND_A * D_MODEL
POOL_WINDOWS = (2, 4, 8, 16)
N_POOL_GROUPS = len(POOL_WINDOWS)
G_A = E_A // N_POOL_GROUPS
N_HEADS_B = D_MODEL // 128
HEAD_DIM_B = 64
V_DIM_B = 2 * HEAD_DIM_B
QK_B = N_HEADS_B * 2 * HEAD_DIM_B
E_B = N_HEADS_B * V_DIM_B
EPS = 1e-6
SUBLN_EPS = 1e-5

kernel_name = "hybrid_pool_diffattn_yoco_trunk"


def lambda_init_fn(layer_idx):
    return 0.8 - 0.6 * math.exp(-0.3 * layer_idx)


def rms_norm(x, g, eps=EPS):
    xf = x.astype(jnp.float32)
    y = xf * lax.rsqrt(jnp.mean(xf * xf, axis=-1, keepdims=True) + eps)
    return (y * g.astype(jnp.float32)).astype(x.dtype)


def modulate(h, shift, scale):
    return h * (1.0 + scale[:, None, :]) + shift[:, None, :]


def pool_mixer(h, w_in, w_group, ch_scale, w_out):
    B, S, _ = h.shape
    u, z = jnp.split(h @ w_in, 2, axis=-1)
    uf = u.astype(jnp.float32).reshape(B, S, N_POOL_GROUPS, G_A)
    cs = jnp.cumsum(uf, axis=1)
    cs = jnp.concatenate([jnp.zeros_like(cs[:, :1]), cs], axis=1)
    t = jnp.arange(S)
    pooled = []
    for g, w in enumerate(POOL_WINDOWS):
        lo = jnp.maximum(t + 1 - w, 0)
        cnt = jnp.minimum(t + 1, w).astype(jnp.float32)
        win_sum = cs[:, 1:, g] - jnp.take(cs[:, :, g], lo, axis=1)
        pooled.append(win_sum / cnt[None, :, None])
    pooled = jnp.stack(pooled, axis=2) - uf
    mixed = jnp.einsum('bsgi,gio->bsgo', pooled.astype(u.dtype), w_group)
    mixed = mixed.reshape(B, S, E_A) * ch_scale
    return (mixed * jax.nn.silu(z)) @ w_out


def shared_kv(x, kv_norm, kv_shift, kv_scale, w_kv):
    B, S, _ = x.shape
    hk = modulate(rms_norm(x, kv_norm), kv_shift, kv_scale)
    kv = hk @ w_kv
    k = kv[..., :QK_B].reshape(B, S, N_HEADS_B, 2, HEAD_DIM_B)
    v = kv[..., QK_B:].reshape(B, S, N_HEADS_B, V_DIM_B)
    return k, v


def diff_attention(h, k, v, w_in, lam_vec, subln_g, w_out, lam_init):
    B, S, _ = h.shape
    q, z = jnp.split(h @ w_in, 2, axis=-1)
    q = q.reshape(B, S, N_HEADS_B, 2, HEAD_DIM_B)
    lv = lam_vec.astype(jnp.float32)
    lam = jnp.exp(jnp.sum(lv[0] * lv[1])) - jnp.exp(jnp.sum(lv[2] * lv[3])) + lam_init
    n_blk = S // Q_BLOCK
    qb = q.reshape(B, n_blk, Q_BLOCK, N_HEADS_B, 2, HEAD_DIM_B).transpose(1, 0, 2, 3, 4, 5)
    key_chunk = jnp.arange(S) // CHUNK
    sm_scale = HEAD_DIM_B ** -0.5

    def block(args):
        q_i, i = args
        s = jnp.einsum('bqhcd,bkhcd->bhcqk', q_i, k).astype(jnp.float32) * sm_scale
        q_chunk = (i * Q_BLOCK + jnp.arange(Q_BLOCK)) // CHUNK
        mask = key_chunk[None, :] <= q_chunk[:, None]
        p = jax.nn.softmax(jnp.where(mask, s, -jnp.inf), axis=-1)
        a = p[:, :, 0] - lam * p[:, :, 1]
        return jnp.einsum('bhqk,bkhv->bqhv', a.astype(v.dtype), v)

    o = lax.map(block, (qb, jnp.arange(n_blk)))
    o = o.transpose(1, 0, 2, 3, 4).reshape(B, S, N_HEADS_B, V_DIM_B)
    o = rms_norm(o, subln_g, SUBLN_EPS) * (1.0 - lam_init)
    y = o.reshape(B, S, E_B) * jax.nn.silu(z)
    return y @ w_out


def setup_inputs(seed: int = 0) -> dict:
    key = jax.random.key(seed)
    ks = jax.random.split(key, 20)
    f32 = jnp.float32
    D = D_MODEL

    def nrm(k, shape, s):
        return jax.random.normal(k, shape, f32) * s

    return {
        "x": nrm(ks[0], (BATCH, SEQ, D), 1.0),
        "c": nrm(ks[1], (BATCH, D), 1.0),
        "ada_w": nrm(ks[2], (DEPTH, D, 3 * D), 0.5 * D ** -0.5),
        "ada_b": nrm(ks[3], (DEPTH, 3 * D), 0.02),
        "norm_pre": 1.0 + nrm(ks[4], (DEPTH, D), 0.05),
        "norm_post": 1.0 + nrm(ks[5], (DEPTH, D), 0.05),
        "a_w_in": nrm(ks[6], (N_A, D, 2 * E_A), D ** -0.5),
        "a_w_group": nrm(ks[7], (N_A, N_POOL_GROUPS, G_A, G_A), G_A ** -0.5),
        "a_scale": 1.0 + nrm(ks[8], (N_A, E_A), 0.1),
        "a_w_out": nrm(ks[9], (N_A, E_A, D), E_A ** -0.5),
        "kv_norm": 1.0 + nrm(ks[10], (D,), 0.05),
        "kv_ada_w": nrm(ks[11], (D, 2 * D), 0.5 * D ** -0.5),
        "kv_ada_b": nrm(ks[12], (2 * D,), 0.02),
        "w_kv": nrm(ks[13], (D, QK_B + E_B), D ** -0.5),
        "b_w_in": nrm(ks[14], (N_B, D, QK_B + E_B), D ** -0.5),
        "b_lambda": nrm(ks[15], (N_B, 4, HEAD_DIM_B), 0.1),
        "b_subln": 1.0 + nrm(ks[16], (N_B, V_DIM_B), 0.05),
        "b_w_out": nrm(ks[17], (N_B, E_B, D), E_B ** -0.5),
    }


def reference(x, c, ada_w, ada_b, norm_pre, norm_post, a_w_in, a_w_group, a_scale,
              a_w_out, kv_norm, kv_ada_w, kv_ada_b, w_kv, b_w_in, b_lambda, b_subln,
              b_w_out):
    cond = jax.nn.silu(c)
    k = v = None
    for l in range(DEPTH):
        shift, scale, gate = jnp.split(cond @ ada_w[l] + ada_b[l], 3, axis=-1)
        h = modulate(rms_norm(x, norm_pre[l]), shift, scale)
        if l < N_A:
            y = pool_mixer(h, a_w_in[l], a_w_group[l], a_scale[l], a_w_out[l])
        else:
            if l == N_A:
                kv_shift, kv_scale = jnp.split(cond @ kv_ada_w + kv_ada_b, 2, axis=-1)
                k, v = shared_kv(x, kv_norm, kv_shift, kv_scale, w_kv)
            j = l - N_A
            y = diff_attention(h, k, v, b_w_in[j], b_lambda[j], b_subln[j], b_w_out[j],
                               lambda_init_fn(l))
        x = x + gate[:, None, :] * rms_norm(y, norm_post[l])
    return x
```

```python
import functools
import math

import jax
import jax.numpy as jnp
from jax import lax
from jax.experimental import pallas as pl
from jax.experimental.pallas import tpu as pltpu

F32 = jnp.float32
BF16 = jnp.bfloat16

CHUNK = 64
POOL_WINDOWS = (2, 4, 8, 16)
HEAD_DIM = 64
V_DIM = 2 * HEAD_DIM
EPS = 1e-6
SUBLN_EPS = 1e-5

POOL_HALO = 16
VMEM_LIMIT_BYTES = 56 * 1024 * 1024


def _lambda_init(layer_idx):
    return 0.8 - 0.6 * math.exp(-0.3 * layer_idx)


def _rms(x, g, eps):
    return x * lax.rsqrt(jnp.mean(x * x, axis=-1, keepdims=True) + eps) * g


def _const_spec(shape):
    zeros = (0,) * len(shape)
    return pl.BlockSpec(shape, lambda *_: zeros, pipeline_mode=pl.Buffered(1))


def _ada_kernel(c_ref, w_ref, b_ref, o_ref):
    cond = jax.nn.silu(c_ref[...]).astype(BF16)
    w = w_ref[0].astype(BF16)
    o_ref[0] = jnp.dot(cond, w, preferred_element_type=F32) + b_ref[0]


def _ada_proj(c, w, b, tn):
    n_layers, d, n = w.shape
    bsz = c.shape[0]
    return pl.pallas_call(
        _ada_kernel,
        grid=(n_layers, n // tn),
        in_specs=[
            pl.BlockSpec((bsz, d), lambda l, j: (0, 0)),
            pl.BlockSpec((1, d, tn), lambda l, j: (l, 0, j)),
            pl.BlockSpec((1, 1, tn), lambda l, j: (l, 0, j)),
        ],
        out_specs=pl.BlockSpec((1, bsz, tn), lambda l, j: (l, 0, j)),
        out_shape=jax.ShapeDtypeStruct((n_layers, bsz, n), F32),
        compiler_params=pltpu.CompilerParams(
            dimension_semantics=("arbitrary", "arbitrary"),
            vmem_limit_bytes=VMEM_LIMIT_BYTES),
        name="ada_proj",
    )(c, w, b)


def _pool_layer_kernel(x_ref, shift_ref, scale_ref, gate_ref, gpre_ref, gpost_ref,
                       win_ref, wg_ref, chs_ref, wout_ref, o_ref, carry_ref, *, tm):
    s_idx = pl.program_id(1)
    e_a = wout_ref.shape[0]
    g_a = e_a // len(POOL_WINDOWS)

    x = x_ref[0]
    h = _rms(x, gpre_ref[...], EPS) * (1.0 + scale_ref[0]) + shift_ref[0]
    uz = jnp.dot(h.astype(BF16), win_ref[...], preferred_element_type=F32)
    u = uz[:, :e_a]
    z = uz[:, e_a:]

    @pl.when(s_idx == 0)
    def _():
        carry_ref[...] = jnp.zeros_like(carry_ref)

    ext = jnp.concatenate([carry_ref[...], u], axis=0)
    carry_ref[...] = u[tm - POOL_HALO:, :]

    t1 = (s_idx * tm + 1 + lax.broadcasted_iota(jnp.int32, (tm, 1), 0)).astype(F32)

    acts = []
    for g, w in enumerate(POOL_WINDOWS):
        cols = slice(g * g_a, (g + 1) * g_a)
        win = ext[:, cols]
        shift = 1
        while shift < w:
            win = win + pltpu.roll(win, shift, axis=0)
            shift *= 2
        inv_cnt = 1.0 / jnp.minimum(t1, float(w))
        pooled = win[POOL_HALO:, :] * inv_cnt - u[:, cols]
        mixed = jnp.dot(pooled.astype(BF16), wg_ref[g], preferred_element_type=F32)
        acts.append((mixed * chs_ref[:, cols] * jax.nn.silu(z[:, cols])).astype(BF16))
    act = jnp.concatenate(acts, axis=-1)
    y = jnp.dot(act, wout_ref[...], preferred_element_type=F32)
    o_ref[0] = x + gate_ref[0] * _rms(y, gpost_ref[...], EPS)


def _pool_layer(x, shift, scale, gate, gpre, gpost, w_in, w_group, ch_scale, w_out, tm):
    bsz, seq, d = x.shape
    e_a = w_out.shape[0]
    vec = pl.BlockSpec((1, 1, d), lambda b, s: (b, 0, 0))
    return pl.pallas_call(
        functools.partial(_pool_layer_kernel, tm=tm),
        grid=(bsz, seq // tm),
        in_specs=[
            pl.BlockSpec((1, tm, d), lambda b, s: (b, s, 0)),
            vec, vec, vec,
            _const_spec((1, d)), _const_spec((1, d)),
            _const_spec(w_in.shape), _const_spec(w_group.shape),
            _const_spec((1, e_a)), _const_spec(w_out.shape),
        ],
        out_specs=pl.BlockSpec((1, tm, d), lambda b, s: (b, s, 0)),
        out_shape=jax.ShapeDtypeStruct(x.shape, F32),
        scratch_shapes=[pltpu.VMEM((POOL_HALO, e_a), F32)],
        compiler_params=pltpu.CompilerParams(
            dimension_semantics=("arbitrary", "arbitrary"),
            vmem_limit_bytes=VMEM_LIMIT_BYTES),
        name="pool_layer",
    )(x, shift, scale, gate, gpre, gpost, w_in, w_group, ch_scale, w_out)


def _kv_kernel(x_ref, shift_ref, scale_ref, g_ref, w_ref, k_ref, v_ref):
    n_heads = k_ref.shape[1]
    qk_w = n_heads * V_DIM
    x = x_ref[0]
    hk = _rms(x, g_ref[...], EPS) * (1.0 + scale_ref[0]) + shift_ref[0]
    kv = jnp.dot(hk.astype(BF16), w_ref[...], preferred_element_type=F32)
    for hd in range(n_heads):
        k_ref[0, hd] = kv[:, hd * V_DIM:(hd + 1) * V_DIM].astype(BF16)
        v_ref[0, hd] = kv[:, qk_w + hd * V_DIM:qk_w + (hd + 1) * V_DIM].astype(BF16)


def _shared_kv(x, shift, scale, g, w_kv, n_heads, tm):
    bsz, seq, d = x.shape
    vec = pl.BlockSpec((1, 1, d), lambda b, s: (b, 0, 0))
    out_sds = jax.ShapeDtypeStruct((bsz, n_heads, seq, V_DIM), BF16)
    out_spec = pl.BlockSpec((1, n_heads, tm, V_DIM), lambda b, s: (b, 0, s, 0))
    return pl.pallas_call(
        _kv_kernel,
        grid=(bsz, seq // tm),
        in_specs=[
            pl.BlockSpec((1, tm, d), lambda b, s: (b, s, 0)),
            vec, vec, _const_spec((1, d)), _const_spec(w_kv.shape),
        ],
        out_specs=(out_spec, out_spec),
        out_shape=(out_sds, out_sds),
        compiler_params=pltpu.CompilerParams(
            dimension_semantics=("arbitrary", "arbitrary"),
            vmem_limit_bytes=VMEM_LIMIT_BYTES),
        name="shared_kv",
    )(x, shift, scale, g, w_kv)


def _attn_layer_kernel(x_ref, shift_ref, scale_ref, gate_ref, gpre_ref, gpost_ref,
                       wq_ref, wz_ref, lam_ref, subln_ref, wout_ref, k_ref, v_ref,
                       o_ref, q_scr, o_scr, m_scr, l_scr, acc_scr, *, tq, lam_init):
    i = pl.program_id(1)
    n_heads = k_ref.shape[1]
    sm_scale = HEAD_DIM ** -0.5

    x = x_ref[0]
    h = (_rms(x, gpre_ref[...], EPS) * (1.0 + scale_ref[0]) + shift_ref[0]).astype(BF16)
    q = jnp.dot(h, wq_ref[...], preferred_element_type=F32) * sm_scale
    for hd in range(n_heads):
        q_scr[hd] = q[:, hd * V_DIM:(hd + 1) * V_DIM].astype(BF16)

    lv = lam_ref[...]
    lam = (jnp.exp(jnp.sum(lv[0:1] * lv[1:2], axis=-1, keepdims=True))
           - jnp.exp(jnp.sum(lv[2:3] * lv[3:4], axis=-1, keepdims=True)) + lam_init)

    lane = lax.broadcasted_iota(jnp.int32, (tq, V_DIM), 1)
    row_chunk = lax.broadcasted_iota(jnp.int32, (tq, tq), 0) // CHUNK
    col_chunk = lax.broadcasted_iota(jnp.int32, (tq, tq), 1) // CHUNK
    diag_mask = col_chunk <= row_chunk

    def kv_block(hd, j, q_c, c, mask):
        k_blk = k_ref[0, hd, pl.ds(pl.multiple_of(j * tq, tq), tq), :]
        v_blk = v_ref[0, hd, pl.ds(pl.multiple_of(j * tq, tq), tq), :]
        s = lax.dot_general(q_c, k_blk, (((1,), (1,)), ((), ())),
                            preferred_element_type=F32)
        if mask is not None:
            s = jnp.where(mask, s, -jnp.inf)
        m_prev = m_scr[c]
        l_prev = l_scr[c]
        m_next = jnp.maximum(m_prev, jnp.max(s, axis=-1, keepdims=True))
        p = jnp.exp(s - m_next[:, 0:1])
        alpha = jnp.exp(m_prev - m_next)
        l_scr[c] = alpha * l_prev + jnp.sum(p, axis=-1, keepdims=True)
        m_scr[c] = m_next
        acc_scr[c] = alpha * acc_scr[c] + jnp.dot(p.astype(BF16), v_blk,
                                                  preferred_element_type=F32)

    def head_body(hd, carry):
        q_h = q_scr[hd]
        zero = jnp.zeros_like(q_h)
        q_maps = (jnp.where(lane < HEAD_DIM, q_h, zero), jnp.where(lane >= HEAD_DIM, q_h, zero))
        m_scr[...] = jnp.full_like(m_scr, -jnp.inf)
        l_scr[...] = jnp.zeros_like(l_scr)
        acc_scr[...] = jnp.zeros_like(acc_scr)

        def kv_body(j, carry2):
            for c in range(2):
                kv_block(hd, j, q_maps[c], c, None)
            return carry2

        lax.fori_loop(0, i, kv_body, 0)
        for c in range(2):
            kv_block(hd, i, q_maps[c], c, diag_mask)

        o = acc_scr[0] / l_scr[0] - lam * (acc_scr[1] / l_scr[1])
        o_scr[hd] = _rms(o, subln_ref[...], SUBLN_EPS) * (1.0 - lam_init)
        return carry

    lax.fori_loop(0, n_heads, head_body, 0)

    z = jnp.dot(h, wz_ref[...], preferred_element_type=F32)
    o_all = jnp.concatenate([o_scr[hd] for hd in range(n_heads)], axis=-1)
    y = jnp.dot((o_all * jax.nn.silu(z)).astype(BF16), wout_ref[...],
                preferred_element_type=F32)
    o_ref[0] = x + gate_ref[0] * _rms(y, gpost_ref[...], EPS)


def _attn_layer(x, shift, scale, gate, gpre, gpost, w_q, w_z, lam_vec, subln, w_out, k, v,
                lam_init, tq):
    bsz, seq, d = x.shape
    n_heads = k.shape[1]
    vec = pl.BlockSpec((1, 1, d), lambda b, s: (b, 0, 0))
    kv_spec = pl.BlockSpec((1, n_heads, seq, V_DIM), lambda b, s: (b, 0, 0, 0))
    return pl.pallas_call(
        functools.partial(_attn_layer_kernel, tq=tq, lam_init=lam_init),
        grid=(bsz, seq // tq),
        in_specs=[
            pl.BlockSpec((1, tq, d), lambda b, s: (b, s, 0)),
            vec, vec, vec,
            _const_spec((1, d)), _const_spec((1, d)),
            _const_spec(w_q.shape), _const_spec(w_z.shape),
            _const_spec(lam_vec.shape), _const_spec((1, V_DIM)), _const_spec(w_out.shape),
            kv_spec, kv_spec,
        ],
        out_specs=pl.BlockSpec((1, tq, d), lambda b, s: (b, s, 0)),
        out_shape=jax.ShapeDtypeStruct(x.shape, F32),
        scratch_shapes=[
            pltpu.VMEM((n_heads, tq, V_DIM), BF16),
            pltpu.VMEM((n_heads, tq, V_DIM), F32),
            pltpu.VMEM((2, tq, V_DIM), F32),
            pltpu.VMEM((2, tq, V_DIM), F32),
            pltpu.VMEM((2, tq, V_DIM), F32),
        ],
        compiler_params=pltpu.CompilerParams(
            dimension_semantics=("arbitrary", "arbitrary"),
            vmem_limit_bytes=VMEM_LIMIT_BYTES),
        name="attn_layer",
    )(x, shift, scale, gate, gpre, gpost, w_q, w_z, lam_vec, subln, w_out, k, v)


def kernel(x, c, ada_w, ada_b, norm_pre, norm_post, a_w_in, a_w_group, a_scale, a_w_out,
           kv_norm, kv_ada_w, kv_ada_b, w_kv, b_w_in, b_lambda, b_subln, b_w_out):
    bsz, seq, d = x.shape
    depth = ada_w.shape[0]
    n_a = a_w_in.shape[0]
    qk_w = w_kv.shape[1] // 2
    n_heads = qk_w // V_DIM

    ada = _ada_proj(c, ada_w, ada_b[:, None, :], tn=d)
    kv_ada = _ada_proj(c, kv_ada_w[None], kv_ada_b[None, None, :], tn=d)[0]

    def vecs(a, n):
        return [a[:, None, j * d:(j + 1) * d] for j in range(n)]

    k = v = None
    for l in range(depth):
        shift, scale, gate = vecs(ada[l], 3)
        gpre = norm_pre[l][None, :]
        gpost = norm_post[l][None, :]
        if l < n_a:
            x = _pool_layer(x, shift, scale, gate, gpre, gpost,
                            a_w_in[l].astype(BF16), a_w_group[l].astype(BF16),
                            a_scale[l][None, :], a_w_out[l].astype(BF16), tm=256)
        else:
            if l == n_a:
                kv_shift, kv_scale = vecs(kv_ada, 2)
                k, v = _shared_kv(x, kv_shift, kv_scale, kv_norm[None, :],
                                  w_kv.astype(BF16), n_heads, tm=512)
            j = l - n_a
            w_in = b_w_in[j].astype(BF16)
            x = _attn_layer(x, shift, scale, gate, gpre, gpost,
                            w_in[:, :qk_w], w_in[:, qk_w:], b_lambda[j], b_subln[j][None, :],
                            b_w_out[j].astype(BF16), k, v, _lambda_init(l), tq=256)
    return x
```

```python
import functools
import math

import jax
import jax.numpy as jnp
from jax import lax
from jax.experimental import pallas as pl
from jax.experimental.pallas import tpu as pltpu

F32 = jnp.float32
BF16 = jnp.bfloat16

CHUNK = 64
POOL_WINDOWS = (2, 4, 8, 16)
HEAD_DIM = 64
V_DIM = 2 * HEAD_DIM
EPS = 1e-6
SUBLN_EPS = 1e-5
LOG2E = math.log2(math.e)

POOL_HALO = 16
POOL_TILE = 256
KV_TILE = 512
ATTN_TILE = 256
QK_AHEAD = 2
S_SLOTS = 2 * QK_AHEAD
VMEM_LIMIT_BYTES = 56 * 1024 * 1024


def _lambda_init(layer_idx):
    return 0.8 - 0.6 * math.exp(-0.3 * layer_idx)


def _rms(x, g, eps):
    return x * lax.rsqrt(jnp.mean(x * x, axis=-1, keepdims=True) + eps) * g


def _const_spec(shape):
    zeros = (0,) * len(shape)
    return pl.BlockSpec(shape, lambda *_: zeros, pipeline_mode=pl.Buffered(1))


_NT_DIMS = (((1,), (1,)), ((), ()))


def _ada_kernel(c_ref, w_ref, b_ref, o_ref):
    cond = jax.nn.silu(c_ref[...]).astype(BF16)
    w = w_ref[0].astype(BF16)
    o_ref[0] = jnp.dot(cond, w, preferred_element_type=F32) + b_ref[0]


def _ada_proj(c, w, b, tn):
    n_layers, d, n = w.shape
    bsz = c.shape[0]
    return pl.pallas_call(
        _ada_kernel,
        grid=(n_layers, n // tn),
        in_specs=[
            pl.BlockSpec((bsz, d), lambda l, j: (0, 0)),
            pl.BlockSpec((1, d, tn), lambda l, j: (l, 0, j)),
            pl.BlockSpec((1, 1, tn), lambda l, j: (l, 0, j)),
        ],
        out_specs=pl.BlockSpec((1, bsz, tn), lambda l, j: (l, 0, j)),
        out_shape=jax.ShapeDtypeStruct((n_layers, bsz, n), F32),
        compiler_params=pltpu.CompilerParams(
            dimension_semantics=("arbitrary", "arbitrary"),
            vmem_limit_bytes=VMEM_LIMIT_BYTES),
        name="ada_proj",
    )(c, w, b)


def _pool_layer_kernel(x_ref, shift_ref, scale_ref, gate_ref, gpre_ref, gpost_ref,
                       win_ref, wg_ref, chs_ref, wout_ref, o_ref, carry_ref, *, tm):
    s_idx = pl.program_id(1)
    e_a = wout_ref.shape[0]
    g_a = e_a // len(POOL_WINDOWS)

    x = x_ref[0]
    h = _rms(x, gpre_ref[...], EPS) * (1.0 + scale_ref[0]) + shift_ref[0]
    uz = jnp.dot(h.astype(BF16), win_ref[...], preferred_element_type=F32)
    u = uz[:, :e_a]
    z = uz[:, e_a:]

    @pl.when(s_idx == 0)
    def _():
        carry_ref[...] = jnp.zeros_like(carry_ref)

    ext = jnp.concatenate([carry_ref[...], u], axis=0)
    carry_ref[...] = u[tm - POOL_HALO:, :]

    t1 = (s_idx * tm + 1 + lax.broadcasted_iota(jnp.int32, (tm, 1), 0)).astype(F32)

    acts = []
    for g, w in enumerate(POOL_WINDOWS):
        cols = slice(g * g_a, (g + 1) * g_a)
        win = ext[:, cols]
        shift = 1
        while shift < w:
            win = win + pltpu.roll(win, shift, axis=0)
            shift *= 2
        inv_cnt = 1.0 / jnp.minimum(t1, float(w))
        pooled = win[POOL_HALO:, :] * inv_cnt - u[:, cols]
        mixed = jnp.dot(pooled.astype(BF16), wg_ref[g], preferred_element_type=F32)
        acts.append((mixed * chs_ref[:, cols] * jax.nn.silu(z[:, cols])).astype(BF16))
    act = jnp.concatenate(acts, axis=-1)
    y = jnp.dot(act, wout_ref[...], preferred_element_type=F32)
    o_ref[0] = x + gate_ref[0] * _rms(y, gpost_ref[...], EPS)


def _pool_layer(x, shift, scale, gate, gpre, gpost, w_in, w_group, ch_scale, w_out):
    bsz, seq, d = x.shape
    e_a = w_out.shape[0]
    tm = POOL_TILE
    vec = pl.BlockSpec((1, 1, d), lambda b, s: (b, 0, 0))
    return pl.pallas_call(
        functools.partial(_pool_layer_kernel, tm=tm),
        grid=(bsz, seq // tm),
        in_specs=[
            pl.BlockSpec((1, tm, d), lambda b, s: (b, s, 0)),
            vec, vec, vec,
            _const_spec((1, d)), _const_spec((1, d)),
            _const_spec(w_in.shape), _const_spec(w_group.shape),
            _const_spec((1, e_a)), _const_spec(w_out.shape),
        ],
        out_specs=pl.BlockSpec((1, tm, d), lambda b, s: (b, s, 0)),
        out_shape=jax.ShapeDtypeStruct(x.shape, F32),
        scratch_shapes=[pltpu.VMEM((POOL_HALO, e_a), F32)],
        compiler_params=pltpu.CompilerParams(
            dimension_semantics=("arbitrary", "arbitrary"),
            vmem_limit_bytes=VMEM_LIMIT_BYTES),
        name="pool_layer",
    )(x, shift, scale, gate, gpre, gpost, w_in, w_group, ch_scale, w_out)


def _kv_kernel(x_ref, shift_ref, scale_ref, g_ref, wk_ref, wvt_ref, k_ref, vt_ref):
    n_heads = k_ref.shape[1]
    n_blk, tk = vt_ref.shape[2], vt_ref.shape[4]
    x = x_ref[0]
    hk = (_rms(x, g_ref[...], EPS) * (1.0 + scale_ref[0]) + shift_ref[0]).astype(BF16)
    k = jnp.dot(hk, wk_ref[...], preferred_element_type=F32)
    vt = lax.dot_general(wvt_ref[...], hk, _NT_DIMS, preferred_element_type=F32)
    for hd in range(n_heads):
        k_ref[0, hd] = k[:, hd * V_DIM:(hd + 1) * V_DIM].astype(BF16)
        for jb in range(n_blk):
            vt_ref[0, hd, jb] = vt[hd * V_DIM:(hd + 1) * V_DIM,
                                   jb * tk:(jb + 1) * tk].astype(BF16)


def _shared_kv(x, shift, scale, g, w_k, w_vt, n_heads):
    bsz, seq, d = x.shape
    tm, tk = KV_TILE, ATTN_TILE
    vec = pl.BlockSpec((1, 1, d), lambda b, s: (b, 0, 0))
    return pl.pallas_call(
        _kv_kernel,
        grid=(bsz, seq // tm),
        in_specs=[
            pl.BlockSpec((1, tm, d), lambda b, s: (b, s, 0)),
            vec, vec, _const_spec((1, d)), _const_spec(w_k.shape), _const_spec(w_vt.shape),
        ],
        out_specs=(
            pl.BlockSpec((1, n_heads, tm, V_DIM), lambda b, s: (b, 0, s, 0)),
            pl.BlockSpec((1, n_heads, tm // tk, V_DIM, tk), lambda b, s: (b, 0, s, 0, 0)),
        ),
        out_shape=(
            jax.ShapeDtypeStruct((bsz, n_heads, seq, V_DIM), BF16),
            jax.ShapeDtypeStruct((bsz, n_heads, seq // tk, V_DIM, tk), BF16),
        ),
        compiler_params=pltpu.CompilerParams(
            dimension_semantics=("arbitrary", "arbitrary"),
            vmem_limit_bytes=VMEM_LIMIT_BYTES),
        name="shared_kv",
    )(x, shift, scale, g, w_k, w_vt)


def _attn_layer_kernel(x_ref, shift_ref, scale_ref, gate_ref, gpre_ref, gpost_ref,
                       wq_ref, wz_ref, lam_ref, subln_ref, wout_ref, k_ref, vt_ref,
                       o_ref, q2_scr, s_scr, m_scr, l_scr, acc_scr, o_scr, *, tq, lam_init):
    i = pl.program_id(1)
    n_heads = k_ref.shape[1]

    x = x_ref[0]
    h = (_rms(x, gpre_ref[...], EPS) * (1.0 + scale_ref[0]) + shift_ref[0]).astype(BF16)
    q = jnp.dot(h, wq_ref[...], preferred_element_type=F32) * (HEAD_DIM ** -0.5 * LOG2E)
    dim_row = lax.broadcasted_iota(jnp.int32, (V_DIM, tq), 0)
    for hd in range(n_heads):
        qt = q[:, hd * V_DIM:(hd + 1) * V_DIM].T
        q2_scr[hd] = jnp.concatenate(
            [jnp.where(dim_row < HEAD_DIM, qt, 0.0), jnp.where(dim_row >= HEAD_DIM, qt, 0.0)],
            axis=1).astype(BF16)

    m_scr[...] = jnp.full_like(m_scr, -jnp.inf)
    l_scr[...] = jnp.zeros_like(l_scr)
    acc_scr[...] = jnp.zeros_like(acc_scr)

    def scores(hd, j):
        k_blk = k_ref[0, hd, pl.ds(pl.multiple_of(j * tq, tq), tq), :]
        s_scr[hd % S_SLOTS] = jnp.dot(k_blk, q2_scr[hd], preferred_element_type=F32)

    def scores_ahead(hd, j, last_block):
        nxt = hd + QK_AHEAD
        if nxt < n_heads:
            scores(nxt, j)
        elif not last_block:
            scores(nxt - n_heads, j + 1)

    def softmax_pv(hd, j, mask):
        s = s_scr[hd % S_SLOTS]
        if mask is not None:
            s = jnp.where(mask, s, -jnp.inf)
        m_prev = m_scr[hd]
        m_next = jnp.maximum(m_prev, jnp.max(s, axis=0, keepdims=True))
        alpha = jnp.exp2(m_prev - m_next)
        p = jnp.exp2(s - m_next)
        l_scr[hd] = alpha * l_scr[hd] + jnp.sum(p, axis=0, keepdims=True)
        m_scr[hd] = m_next
        acc_scr[hd] = alpha * acc_scr[hd] + jnp.dot(
            vt_ref[0, hd, j], p.astype(BF16), preferred_element_type=F32)

    for hd in range(QK_AHEAD):
        scores(hd, 0)

    def kv_body(j, carry):
        for hd in range(n_heads):
            scores_ahead(hd, j, last_block=False)
            softmax_pv(hd, j, None)
        return carry

    lax.fori_loop(0, i, kv_body, 0)

    key_chunk = lax.broadcasted_iota(jnp.int32, (tq, 2 * tq), 0) // CHUNK
    qry = lax.broadcasted_iota(jnp.int32, (tq, 2 * tq), 1)
    qry_chunk = jnp.where(qry >= tq, qry - tq, qry) // CHUNK
    diag_mask = key_chunk <= qry_chunk
    for hd in range(n_heads):
        scores_ahead(hd, i, last_block=True)
        softmax_pv(hd, i, diag_mask)

    lv = lam_ref[...]
    lam = (jnp.exp(jnp.sum(lv[0:1] * lv[1:2], axis=-1, keepdims=True))
           - jnp.exp(jnp.sum(lv[2:3] * lv[3:4], axis=-1, keepdims=True)) + lam_init)
    for hd in range(n_heads):
        a = acc_scr[hd] * (1.0 / l_scr[hd])
        o_t = a[:, :tq] - lam * a[:, tq:]
        o_t = o_t * lax.rsqrt(jnp.mean(o_t * o_t, axis=0, keepdims=True) + SUBLN_EPS)
        o_scr[:, hd * V_DIM:(hd + 1) * V_DIM] = o_t.T * subln_ref[...] * (1.0 - lam_init)

    z = jnp.dot(h, wz_ref[...], preferred_element_type=F32)
    y = jnp.dot((o_scr[...] * jax.nn.silu(z)).astype(BF16), wout_ref[...],
                preferred_element_type=F32)
    o_ref[0] = x + gate_ref[0] * _rms(y, gpost_ref[...], EPS)


def _attn_layer(x, shift, scale, gate, gpre, gpost, w_q, w_z, lam_vec, subln, w_out, k, vt,
                lam_init):
    bsz, seq, d = x.shape
    n_heads = k.shape[1]
    tq = ATTN_TILE
    vec = pl.BlockSpec((1, 1, d), lambda b, s: (b, 0, 0))
    return pl.pallas_call(
        functools.partial(_attn_layer_kernel, tq=tq, lam_init=lam_init),
        grid=(bsz, seq // tq),
        in_specs=[
            pl.BlockSpec((1, tq, d), lambda b, s: (b, s, 0)),
            vec, vec, vec,
            _const_spec((1, d)), _const_spec((1, d)),
            _const_spec(w_q.shape), _const_spec(w_z.shape),
            _const_spec(lam_vec.shape), _const_spec((1, V_DIM)), _const_spec(w_out.shape),
            pl.BlockSpec((1,) + k.shape[1:], lambda b, s: (b, 0, 0, 0)),
            pl.BlockSpec((1,) + vt.shape[1:], lambda b, s: (b, 0, 0, 0, 0)),
        ],
        out_specs=pl.BlockSpec((1, tq, d), lambda b, s: (b, s, 0)),
        out_shape=jax.ShapeDtypeStruct(x.shape, F32),
        scratch_shapes=[
            pltpu.VMEM((n_heads, V_DIM, 2 * tq), BF16),
            pltpu.VMEM((S_SLOTS, tq, 2 * tq), F32),
            pltpu.VMEM((n_heads, 1, 2 * tq), F32),
            pltpu.VMEM((n_heads, 1, 2 * tq), F32),
            pltpu.VMEM((n_heads, V_DIM, 2 * tq), F32),
            pltpu.VMEM((tq, n_heads * V_DIM), F32),
        ],
        compiler_params=pltpu.CompilerParams(
            dimension_semantics=("arbitrary", "arbitrary"),
            vmem_limit_bytes=VMEM_LIMIT_BYTES),
        name="attn_layer",
    )(x, shift, scale, gate, gpre, gpost, w_q, w_z, lam_vec, subln, w_out, k, vt)


def kernel(x, c, ada_w, ada_b, norm_pre, norm_post, a_w_in, a_w_group, a_scale, a_w_out,
           kv_norm, kv_ada_w, kv_ada_b, w_kv, b_w_in, b_lambda, b_subln, b_w_out):
    bsz, seq, d = x.shape
    depth = ada_w.shape[0]
    n_a = a_w_in.shape[0]
    qk_w = w_kv.shape[1] // 2
    n_heads = qk_w // V_DIM

    ada = _ada_proj(c, ada_w, ada_b[:, None, :], tn=d)
    kv_ada = _ada_proj(c, kv_ada_w[None], kv_ada_b[None, None, :], tn=d)[0]

    def vecs(a, n):
        return [a[:, None, j * d:(j + 1) * d] for j in range(n)]

    k = vt = None
    for l in range(depth):
        shift, scale, gate = vecs(ada[l], 3)
        gpre = norm_pre[l][None, :]
        gpost = norm_post[l][None, :]
        if l < n_a:
            x = _pool_layer(x, shift, scale, gate, gpre, gpost,
                            a_w_in[l].astype(BF16), a_w_group[l].astype(BF16),
                            a_scale[l][None, :], a_w_out[l].astype(BF16))
        else:
            if l == n_a:
                kv_shift, kv_scale = vecs(kv_ada, 2)
                w_kv16 = w_kv.astype(BF16)
                k, vt = _shared_kv(x, kv_shift, kv_scale, kv_norm[None, :],
                                   w_kv16[:, :qk_w], w_kv16[:, qk_w:].T, n_heads)
            j = l - n_a
            w_in = b_w_in[j].astype(BF16)
            x = _attn_layer(x, shift, scale, gate, gpre, gpost,
                            w_in[:, :qk_w], w_in[:, qk_w:], b_lambda[j], b_subln[j][None, :],
                            b_w_out[j].astype(BF16), k, vt, _lambda_init(l))
    return x
```

```python
import functools
import math

import jax
import jax.numpy as jnp
from jax import lax
from jax.experimental import pallas as pl
from jax.experimental.pallas import tpu as pltpu

F32 = jnp.float32
BF16 = jnp.bfloat16

CHUNK = 64
POOL_WINDOWS = (2, 4, 8, 16)
HEAD_DIM = 64
V_DIM = 2 * HEAD_DIM
EPS = 1e-6
SUBLN_EPS = 1e-5
LOG2E = math.log2(math.e)

LANES = 128
BF16_ROWS = 16
ACC_ROWS = V_DIM + BF16_ROWS

POOL_HALO = 16
POOL_TILE = 512
POOL_SUBTILE = 256
KV_TILE = 512
ATTN_TILE = 256
QK_AHEAD = 2
S_SLOTS = 2 * QK_AHEAD
VMEM_LIMIT_BYTES = 56 * 1024 * 1024


def _lambda_init(layer_idx):
    return 0.8 - 0.6 * math.exp(-0.3 * layer_idx)


def _rms(x, g, eps):
    return x * lax.rsqrt(jnp.mean(x * x, axis=-1, keepdims=True) + eps) * g


def _const_spec(shape):
    zeros = (0,) * len(shape)
    return pl.BlockSpec(shape, lambda *_: zeros, pipeline_mode=pl.Buffered(1))


_NT_DIMS = (((1,), (1,)), ((), ()))


def _ada_kernel(c_ref, w_ref, b_ref, o_ref):
    cond = jax.nn.silu(c_ref[...]).astype(BF16)
    w = w_ref[0].astype(BF16)
    o_ref[0] = jnp.dot(cond, w, preferred_element_type=F32) + b_ref[0]


def _ada_proj(c, w, b, tn):
    n_layers, d, n = w.shape
    bsz = c.shape[0]
    return pl.pallas_call(
        _ada_kernel,
        grid=(n_layers, n // tn),
        in_specs=[
            pl.BlockSpec((bsz, d), lambda l, j: (0, 0)),
            pl.BlockSpec((1, d, tn), lambda l, j: (l, 0, j)),
            pl.BlockSpec((1, 1, tn), lambda l, j: (l, 0, j)),
        ],
        out_specs=pl.BlockSpec((1, bsz, tn), lambda l, j: (l, 0, j)),
        out_shape=jax.ShapeDtypeStruct((n_layers, bsz, n), F32),
        compiler_params=pltpu.CompilerParams(
            dimension_semantics=("arbitrary", "arbitrary"),
            vmem_limit_bytes=VMEM_LIMIT_BYTES),
        name="ada_proj",
    )(c, w, b)


def _pool_layer_kernel(x_ref, shift_ref, scale_ref, gate_ref, gpre_ref, gpost_ref,
                       win_ref, wg_ref, chs_ref, wout_ref, o_ref, carry_ref, *, tm, sub):
    s_idx = pl.program_id(1)
    e_a = wout_ref.shape[0]
    g_a = e_a // len(POOL_WINDOWS)
    n_sub = tm // sub

    @pl.when(s_idx == 0)
    def _():
        carry_ref[...] = jnp.zeros_like(carry_ref)

    gain = gpre_ref[...] * (1.0 + scale_ref[0])
    xs, us, zs = [], [], []
    for r in range(n_sub):
        x = x_ref[0, r * sub:(r + 1) * sub, :]
        h = x * lax.rsqrt(jnp.mean(x * x, axis=-1, keepdims=True) + EPS) * gain + shift_ref[0]
        uz = jnp.dot(h.astype(BF16), win_ref[...], preferred_element_type=F32)
        xs.append(x)
        us.append(uz[:, :e_a])
        zs.append(uz[:, e_a:])

    acts = []
    for r in range(n_sub):
        u, z = us[r], zs[r]
        prev = carry_ref[...] if r == 0 else us[r - 1][sub - POOL_HALO:, :]
        ext = jnp.concatenate([prev, u], axis=0)
        t1 = (s_idx * tm + r * sub + 1
              + lax.broadcasted_iota(jnp.int32, (sub, 1), 0)).astype(F32)
        parts = []
        for g, w in enumerate(POOL_WINDOWS):
            cols = slice(g * g_a, (g + 1) * g_a)
            win = ext[:, cols]
            shift = 1
            while shift < w:
                win = win + pltpu.roll(win, shift, axis=0)
                shift *= 2
            inv_cnt = 1.0 / jnp.minimum(t1, float(w))
            pooled = win[POOL_HALO:, :] * inv_cnt - u[:, cols]
            mixed = jnp.dot(pooled.astype(BF16), wg_ref[g], preferred_element_type=F32)
            parts.append((mixed * chs_ref[:, cols] * jax.nn.silu(z[:, cols])).astype(BF16))
        acts.append(jnp.concatenate(parts, axis=-1))
    carry_ref[...] = us[-1][sub - POOL_HALO:, :]

    for r in range(n_sub):
        y = jnp.dot(acts[r], wout_ref[...], preferred_element_type=F32)
        o_ref[0, r * sub:(r + 1) * sub, :] = xs[r] + gate_ref[0] * _rms(y, gpost_ref[...], EPS)


def _pool_layer(x, shift, scale, gate, gpre, gpost, w_in, w_group, ch_scale, w_out):
    bsz, seq, d = x.shape
    e_a = w_out.shape[0]
    tm = POOL_TILE
    vec = pl.BlockSpec((1, 1, d), lambda b, s: (b, 0, 0))
    return pl.pallas_call(
        functools.partial(_pool_layer_kernel, tm=tm, sub=POOL_SUBTILE),
        grid=(bsz, seq // tm),
        in_specs=[
            pl.BlockSpec((1, tm, d), lambda b, s: (b, s, 0)),
            vec, vec, vec,
            _const_spec((1, d)), _const_spec((1, d)),
            _const_spec(w_in.shape), _const_spec(w_group.shape),
            _const_spec((1, e_a)), _const_spec(w_out.shape),
        ],
        out_specs=pl.BlockSpec((1, tm, d), lambda b, s: (b, s, 0)),
        out_shape=jax.ShapeDtypeStruct(x.shape, F32),
        scratch_shapes=[pltpu.VMEM((POOL_HALO, e_a), F32)],
        compiler_params=pltpu.CompilerParams(
            dimension_semantics=("arbitrary", "arbitrary"),
            vmem_limit_bytes=VMEM_LIMIT_BYTES),
        name="pool_layer",
    )(x, shift, scale, gate, gpre, gpost, w_in, w_group, ch_scale, w_out)


def _kv_kernel(x_ref, shift_ref, scale_ref, g_ref, wk_ref, wvt_ref, k_ref, vt_ref):
    n_heads = k_ref.shape[1]
    n_blk, tk = vt_ref.shape[2], vt_ref.shape[4]
    x = x_ref[0]
    hk = (_rms(x, g_ref[...], EPS) * (1.0 + scale_ref[0]) + shift_ref[0]).astype(BF16)
    k = jnp.dot(hk, wk_ref[...], preferred_element_type=F32)
    vt = lax.dot_general(wvt_ref[...], hk, _NT_DIMS, preferred_element_type=F32)
    for hd in range(n_heads):
        k_ref[0, hd] = k[:, hd * V_DIM:(hd + 1) * V_DIM].astype(BF16)
        for jb in range(n_blk):
            vt_ref[0, hd, jb] = vt[hd * V_DIM:(hd + 1) * V_DIM,
                                   jb * tk:(jb + 1) * tk].astype(BF16)


def _shared_kv(x, shift, scale, g, w_k, w_vt, n_heads):
    bsz, seq, d = x.shape
    tm, tk = KV_TILE, ATTN_TILE
    vec = pl.BlockSpec((1, 1, d), lambda b, s: (b, 0, 0))
    return pl.pallas_call(
        _kv_kernel,
        grid=(bsz, seq // tm),
        in_specs=[
            pl.BlockSpec((1, tm, d), lambda b, s: (b, s, 0)),
            vec, vec, _const_spec((1, d)), _const_spec(w_k.shape), _const_spec(w_vt.shape),
        ],
        out_specs=(
            pl.BlockSpec((1, n_heads, tm, V_DIM), lambda b, s: (b, 0, s, 0)),
            pl.BlockSpec((1, n_heads, tm // tk, V_DIM, tk), lambda b, s: (b, 0, s, 0, 0)),
        ),
        out_shape=(
            jax.ShapeDtypeStruct((bsz, n_heads, seq, V_DIM), BF16),
            jax.ShapeDtypeStruct((bsz, n_heads, seq // tk, V_DIM, tk), BF16),
        ),
        compiler_params=pltpu.CompilerParams(
            dimension_semantics=("arbitrary", "arbitrary"),
            vmem_limit_bytes=VMEM_LIMIT_BYTES),
        name="shared_kv",
    )(x, shift, scale, g, w_k, w_vt)


def _attn_layer_kernel(x_ref, shift_ref, scale_ref, gate_ref, gpre_ref, gpost_ref,
                       wq_ref, wz_ref, lam_ref, subln_ref, wout_ref, k_ref, vt_ref,
                       o_ref, q2_scr, s_scr, m_scr, acc_scr, o_scr, *, tq, lam_init):
    i = pl.program_id(1)
    n_heads = k_ref.shape[1]

    x = x_ref[0]
    gain = gpre_ref[...] * (1.0 + scale_ref[0])
    h = (x * lax.rsqrt(jnp.mean(x * x, axis=-1, keepdims=True) + EPS) * gain
         + shift_ref[0]).astype(BF16)
    q = jnp.dot(h, wq_ref[...], preferred_element_type=F32) * (HEAD_DIM ** -0.5 * LOG2E)
    dim_row = lax.broadcasted_iota(jnp.int32, (V_DIM, tq), 0)
    for hd in range(n_heads):
        qt = q[:, hd * V_DIM:(hd + 1) * V_DIM].T
        q2_scr[hd] = jnp.concatenate(
            [jnp.where(dim_row < HEAD_DIM, qt, 0.0), jnp.where(dim_row >= HEAD_DIM, qt, 0.0)],
            axis=1).astype(BF16)

    m_scr[...] = jnp.full_like(m_scr, -jnp.inf)
    acc_scr[...] = jnp.zeros_like(acc_scr)

    ones_rows = (lax.broadcasted_iota(jnp.int32, (ACC_ROWS - V_DIM, tq), 0) == 0).astype(BF16)

    chunks_per_lb = LANES // CHUNK
    lane_chunk = lax.broadcasted_iota(jnp.int32, (CHUNK, LANES), 1) // CHUNK

    def diag_pieces(s):
        pieces = {}
        for kc in range(tq // CHUNK):
            for lb in range(2 * tq // LANES):
                first_qc = (lb % (tq // LANES)) * chunks_per_lb
                if kc >= first_qc + chunks_per_lb:
                    continue
                pc = s[kc * CHUNK:(kc + 1) * CHUNK, lb * LANES:(lb + 1) * LANES]
                if kc > first_qc:
                    pc = jnp.where(lane_chunk + first_qc >= kc, pc, -jnp.inf)
                pieces[kc, lb] = pc
        return pieces

    def scores(hd, j):
        k_blk = k_ref[0, hd, pl.ds(pl.multiple_of(j * tq, tq), tq), :]
        s_scr[hd % S_SLOTS] = jnp.dot(k_blk, q2_scr[hd], preferred_element_type=F32)

    def scores_ahead(hd, j, last_block):
        nxt = hd + QK_AHEAD
        if nxt < n_heads:
            scores(nxt, j)
        elif not last_block:
            scores(nxt - n_heads, j + 1)

    def softmax_pv(hd, j, diag):
        s = s_scr[hd % S_SLOTS]
        m_prev = m_scr[hd]
        if diag:
            pieces = diag_pieces(s)
            m_blk = jnp.concatenate(
                [functools.reduce(jnp.maximum, [jnp.max(pc, axis=0, keepdims=True)
                                                for (_, lb_), pc in pieces.items() if lb_ == lb])
                 for lb in range(2 * tq // LANES)], axis=1)
        else:
            m_blk = jnp.max(s, axis=0, keepdims=True)
        m_next = jnp.maximum(m_prev, m_blk)
        alpha = jnp.exp2(m_prev - m_next)
        if diag:
            p = jnp.concatenate(
                [jnp.concatenate(
                    [jnp.exp2(pieces[kc, lb] - m_next[:, lb * LANES:(lb + 1) * LANES]).astype(BF16)
                     if (kc, lb) in pieces else jnp.zeros((CHUNK, LANES), BF16)
                     for lb in range(2 * tq // LANES)], axis=1)
                 for kc in range(tq // CHUNK)], axis=0)
        else:
            p = jnp.exp2(s - m_next).astype(BF16)
        m_scr[hd] = m_next
        vt_ext = jnp.concatenate([vt_ref[0, hd, j], ones_rows], axis=0)
        acc_scr[hd] = alpha * acc_scr[hd] + jnp.dot(vt_ext, p, preferred_element_type=F32)

    for hd in range(QK_AHEAD):
        scores(hd, 0)

    def kv_body(j, carry):
        for hd in range(n_heads):
            scores_ahead(hd, j, last_block=False)
            softmax_pv(hd, j, diag=False)
        return carry

    lax.fori_loop(0, i, kv_body, 0)

    for hd in range(n_heads):
        scores_ahead(hd, i, last_block=True)
        softmax_pv(hd, i, diag=True)

    z = jnp.dot(h, wz_ref[...], preferred_element_type=F32)

    lv = lam_ref[...]
    lam = (jnp.exp(jnp.sum(lv[0:1] * lv[1:2], axis=-1, keepdims=True))
           - jnp.exp(jnp.sum(lv[2:3] * lv[3:4], axis=-1, keepdims=True)) + lam_init)
    for hd in range(n_heads):
        acc = acc_scr[hd]
        a = acc[:V_DIM] * (1.0 / acc[V_DIM:V_DIM + 1])
        o_t = a[:, :tq] - lam * a[:, tq:]
        o_t = o_t * lax.rsqrt(jnp.mean(o_t * o_t, axis=0, keepdims=True) + SUBLN_EPS)
        o_scr[:, hd * V_DIM:(hd + 1) * V_DIM] = o_t.T * subln_ref[...] * (1.0 - lam_init)

    y = jnp.dot((o_scr[...] * jax.nn.silu(z)).astype(BF16), wout_ref[...],
                preferred_element_type=F32)
    o_ref[0] = x + gate_ref[0] * _rms(y, gpost_ref[...], EPS)


def _attn_layer(x, shift, scale, gate, gpre, gpost, w_q, w_z, lam_vec, subln, w_out, k, vt,
                lam_init):
    bsz, seq, d = x.shape
    n_heads = k.shape[1]
    tq = ATTN_TILE
    vec = pl.BlockSpec((1, 1, d), lambda b, s: (b, 0, 0))
    return pl.pallas_call(
        functools.partial(_attn_layer_kernel, tq=tq, lam_init=lam_init),
        grid=(bsz, seq // tq),
        in_specs=[
            pl.BlockSpec((1, tq, d), lambda b, s: (b, s, 0)),
            vec, vec, vec,
            _const_spec((1, d)), _const_spec((1, d)),
            _const_spec(w_q.shape), _const_spec(w_z.shape),
            _const_spec(lam_vec.shape), _const_spec((1, V_DIM)), _const_spec(w_out.shape),
            pl.BlockSpec((1,) + k.shape[1:], lambda b, s: (b, 0, 0, 0)),
            pl.BlockSpec((1,) + vt.shape[1:], lambda b, s: (b, 0, 0, 0, 0)),
        ],
        out_specs=pl.BlockSpec((1, tq, d), lambda b, s: (b, s, 0)),
        out_shape=jax.ShapeDtypeStruct(x.shape, F32),
        scratch_shapes=[
            pltpu.VMEM((n_heads, V_DIM, 2 * tq), BF16),
            pltpu.VMEM((S_SLOTS, tq, 2 * tq), F32),
            pltpu.VMEM((n_heads, 1, 2 * tq), F32),
            pltpu.VMEM((n_heads, ACC_ROWS, 2 * tq), F32),
            pltpu.VMEM((tq, n_heads * V_DIM), F32),
        ],
        compiler_params=pltpu.CompilerParams(
            dimension_semantics=("arbitrary", "arbitrary"),
            vmem_limit_bytes=VMEM_LIMIT_BYTES),
        name="attn_layer",
    )(x, shift, scale, gate, gpre, gpost, w_q, w_z, lam_vec, subln, w_out, k, vt)


def kernel(x, c, ada_w, ada_b, norm_pre, norm_post, a_w_in, a_w_group, a_scale, a_w_out,
           kv_norm, kv_ada_w, kv_ada_b, w_kv, b_w_in, b_lambda, b_subln, b_w_out):
    bsz, seq, d = x.shape
    depth = ada_w.shape[0]
    n_a = a_w_in.shape[0]
    qk_w = w_kv.shape[1] // 2
    n_heads = qk_w // V_DIM

    ada = _ada_proj(c, ada_w, ada_b[:, None, :], tn=d)
    kv_ada = _ada_proj(c, kv_ada_w[None], kv_ada_b[None, None, :], tn=d)[0]

    def vecs(a, n):
        return [a[:, None, j * d:(j + 1) * d] for j in range(n)]

    k = vt = None
    for l in range(depth):
        shift, scale, gate = vecs(ada[l], 3)
        gpre = norm_pre[l][None, :]
        gpost = norm_post[l][None, :]
        if l < n_a:
            x = _pool_layer(x, shift, scale, gate, gpre, gpost,
                            a_w_in[l].astype(BF16), a_w_group[l].astype(BF16),
                            a_scale[l][None, :], a_w_out[l].astype(BF16))
        else:
            if l == n_a:
                kv_shift, kv_scale = vecs(kv_ada, 2)
                w_kv16 = w_kv.astype(BF16)
                k, vt = _shared_kv(x, kv_shift, kv_scale, kv_norm[None, :],
                                   w_kv16[:, :qk_w], w_kv16[:, qk_w:].T, n_heads)
            j = l - n_a
            w_in = b_w_in[j].astype(BF16)
            x = _attn_layer(x, shift, scale, gate, gpre, gpost,
                            w_in[:, :qk_w], w_in[:, qk_w:], b_lambda[j], b_subln[j][None, :],
                            b_w_out[j].astype(BF16), k, vt, _lambda_init(l))
    return x
```

```python
import functools
import math

import jax
import jax.numpy as jnp
from jax import lax
from jax.experimental import pallas as pl
from jax.experimental.pallas import tpu as pltpu

F32 = jnp.float32
BF16 = jnp.bfloat16

CHUNK = 64
POOL_WINDOWS = (2, 4, 8, 16)
HEAD_DIM = 64
V_DIM = 2 * HEAD_DIM
EPS = 1e-6
SUBLN_EPS = 1e-5
LOG2E = math.log2(math.e)

LANES = 128
BF16_ROWS = 16
ACC_ROWS = V_DIM + BF16_ROWS

POOL_HALO = 16
POOL_TILE = 512
POOL_SUBTILE = 256
KV_TILE = 512
ATTN_TILE = 512
ATTN_BLOCK = 256
QK_AHEAD = 2
S_SLOTS = 4
VMEM_LIMIT_BYTES = 56 * 1024 * 1024


def _lambda_init(layer_idx):
    return 0.8 - 0.6 * math.exp(-0.3 * layer_idx)


def _rms(x, g, eps):
    return x * lax.rsqrt(jnp.mean(x * x, axis=-1, keepdims=True) + eps) * g


def _const_spec(shape):
    zeros = (0,) * len(shape)
    return pl.BlockSpec(shape, lambda *_: zeros, pipeline_mode=pl.Buffered(1))


_NT_DIMS = (((1,), (1,)), ((), ()))


def _ada_kernel(c_ref, w_ref, b_ref, o_ref):
    cond = jax.nn.silu(c_ref[...]).astype(BF16)
    w = w_ref[0].astype(BF16)
    o_ref[0] = jnp.dot(cond, w, preferred_element_type=F32) + b_ref[0]


def _ada_proj(c, w, b, tn):
    n_layers, d, n = w.shape
    bsz = c.shape[0]
    return pl.pallas_call(
        _ada_kernel,
        grid=(n_layers, n // tn),
        in_specs=[
            pl.BlockSpec((bsz, d), lambda l, j: (0, 0)),
            pl.BlockSpec((1, d, tn), lambda l, j: (l, 0, j)),
            pl.BlockSpec((1, 1, tn), lambda l, j: (l, 0, j)),
        ],
        out_specs=pl.BlockSpec((1, bsz, tn), lambda l, j: (l, 0, j)),
        out_shape=jax.ShapeDtypeStruct((n_layers, bsz, n), F32),
        compiler_params=pltpu.CompilerParams(
            dimension_semantics=("arbitrary", "arbitrary"),
            vmem_limit_bytes=VMEM_LIMIT_BYTES),
        name="ada_proj",
    )(c, w, b)


def _pool_layer_kernel(x_ref, shift_ref, scale_ref, gate_ref, gpre_ref, gpost_ref,
                       win_ref, wg_ref, chs_ref, wout_ref, o_ref, carry_ref, *, tm, sub):
    s_idx = pl.program_id(1)
    e_a = wout_ref.shape[0]
    g_a = e_a // len(POOL_WINDOWS)
    n_sub = tm // sub

    @pl.when(s_idx == 0)
    def _():
        carry_ref[...] = jnp.zeros_like(carry_ref)

    gain = gpre_ref[...] * (1.0 + scale_ref[0])
    xs, us, zs = [], [], []
    for r in range(n_sub):
        x = x_ref[0, r * sub:(r + 1) * sub, :]
        h = x * lax.rsqrt(jnp.mean(x * x, axis=-1, keepdims=True) + EPS) * gain + shift_ref[0]
        uz = jnp.dot(h.astype(BF16), win_ref[...], preferred_element_type=F32)
        xs.append(x)
        us.append(uz[:, :e_a])
        zs.append(uz[:, e_a:])

    acts = []
    for r in range(n_sub):
        u, z = us[r], zs[r]
        prev = carry_ref[...] if r == 0 else us[r - 1][sub - POOL_HALO:, :]
        ext = jnp.concatenate([prev, u], axis=0)
        t1 = (s_idx * tm + r * sub + 1
              + lax.broadcasted_iota(jnp.int32, (sub, 1), 0)).astype(F32)
        parts = []
        for g, w in enumerate(POOL_WINDOWS):
            cols = slice(g * g_a, (g + 1) * g_a)
            win = ext[:, cols]
            shift = 1
            while shift < w:
                win = win + pltpu.roll(win, shift, axis=0)
                shift *= 2
            inv_cnt = 1.0 / jnp.minimum(t1, float(w))
            pooled = win[POOL_HALO:, :] * inv_cnt - u[:, cols]
            mixed = jnp.dot(pooled.astype(BF16), wg_ref[g], preferred_element_type=F32)
            parts.append((mixed * chs_ref[:, cols] * jax.nn.silu(z[:, cols])).astype(BF16))
        acts.append(jnp.concatenate(parts, axis=-1))
    carry_ref[...] = us[-1][sub - POOL_HALO:, :]

    for r in range(n_sub):
        y = jnp.dot(acts[r], wout_ref[...], preferred_element_type=F32)
        o_ref[0, r * sub:(r + 1) * sub, :] = xs[r] + gate_ref[0] * _rms(y, gpost_ref[...], EPS)


def _pool_layer(x, shift, scale, gate, gpre, gpost, w_in, w_group, ch_scale, w_out):
    bsz, seq, d = x.shape
    e_a = w_out.shape[0]
    tm = POOL_TILE
    vec = pl.BlockSpec((1, 1, d), lambda b, s: (b, 0, 0))
    return pl.pallas_call(
        functools.partial(_pool_layer_kernel, tm=tm, sub=POOL_SUBTILE),
        grid=(bsz, seq // tm),
        in_specs=[
            pl.BlockSpec((1, tm, d), lambda b, s: (b, s, 0)),
            vec, vec, vec,
            _const_spec((1, d)), _const_spec((1, d)),
            _const_spec(w_in.shape), _const_spec(w_group.shape),
            _const_spec((1, e_a)), _const_spec(w_out.shape),
        ],
        out_specs=pl.BlockSpec((1, tm, d), lambda b, s: (b, s, 0)),
        out_shape=jax.ShapeDtypeStruct(x.shape, F32),
        scratch_shapes=[pltpu.VMEM((POOL_HALO, e_a), F32)],
        compiler_params=pltpu.CompilerParams(
            dimension_semantics=("arbitrary", "arbitrary"),
            vmem_limit_bytes=VMEM_LIMIT_BYTES),
        name="pool_layer",
    )(x, shift, scale, gate, gpre, gpost, w_in, w_group, ch_scale, w_out)


def _kv_kernel(x_ref, shift_ref, scale_ref, g_ref, wk_ref, wvt_ref, k_ref, vt_ref):
    n_heads = k_ref.shape[1]
    n_blk, tk = vt_ref.shape[2], vt_ref.shape[4]
    gain = g_ref[...] * (1.0 + scale_ref[0])
    for jb in range(n_blk):
        x = x_ref[0, jb * tk:(jb + 1) * tk, :]
        hk = (x * lax.rsqrt(jnp.mean(x * x, axis=-1, keepdims=True) + EPS) * gain
              + shift_ref[0]).astype(BF16)
        k = jnp.dot(hk, wk_ref[...], preferred_element_type=F32)
        vt = lax.dot_general(wvt_ref[...], hk, _NT_DIMS, preferred_element_type=F32)
        for hd in range(n_heads):
            k_ref[0, hd, jb * tk:(jb + 1) * tk, :] = k[:, hd * V_DIM:(hd + 1) * V_DIM].astype(BF16)
            vt_ref[0, hd, jb] = vt[hd * V_DIM:(hd + 1) * V_DIM, :].astype(BF16)


def _shared_kv(x, shift, scale, g, w_k, w_vt, n_heads):
    bsz, seq, d = x.shape
    tm, tk = KV_TILE, ATTN_BLOCK
    vec = pl.BlockSpec((1, 1, d), lambda b, s: (b, 0, 0))
    return pl.pallas_call(
        _kv_kernel,
        grid=(bsz, seq // tm),
        in_specs=[
            pl.BlockSpec((1, tm, d), lambda b, s: (b, s, 0)),
            vec, vec, _const_spec((1, d)), _const_spec(w_k.shape), _const_spec(w_vt.shape),
        ],
        out_specs=(
            pl.BlockSpec((1, n_heads, tm, V_DIM), lambda b, s: (b, 0, s, 0)),
            pl.BlockSpec((1, n_heads, tm // tk, V_DIM, tk), lambda b, s: (b, 0, s, 0, 0)),
        ),
        out_shape=(
            jax.ShapeDtypeStruct((bsz, n_heads, seq, V_DIM), BF16),
            jax.ShapeDtypeStruct((bsz, n_heads, seq // tk, V_DIM, tk), BF16),
        ),
        compiler_params=pltpu.CompilerParams(
            dimension_semantics=("arbitrary", "arbitrary"),
            vmem_limit_bytes=VMEM_LIMIT_BYTES),
        name="shared_kv",
    )(x, shift, scale, g, w_k, w_vt)


def _attn_layer_kernel(x_ref, shift_ref, scale_ref, gate_ref, gpre_ref, gpost_ref,
                       wq_ref, wz_ref, lam_ref, subln_ref, wout_ref, k_ref, vt_ref,
                       o_ref, q2_scr, s_scr, m_scr, acc_scr, o_scr, *, tq, tk, lam_init):
    i = pl.program_id(1)
    n_heads = k_ref.shape[1]
    n_sub = tq // tk
    sub_w = 2 * tk
    width = n_sub * sub_w

    def rows(r):
        return slice(r * tk, (r + 1) * tk)

    gain = gpre_ref[...] * (1.0 + scale_ref[0])
    hs, qs = [], []
    for r in range(n_sub):
        x = x_ref[0, rows(r), :]
        hs.append((x * lax.rsqrt(jnp.mean(x * x, axis=-1, keepdims=True) + EPS) * gain
                   + shift_ref[0]).astype(BF16))
        qs.append(jnp.dot(hs[r], wq_ref[...], preferred_element_type=F32)
                  * (HEAD_DIM ** -0.5 * LOG2E))
    dim_row = lax.broadcasted_iota(jnp.int32, (V_DIM, tk), 0)
    for r in range(n_sub):
        q = qs[r]
        for hd in range(n_heads):
            qt = q[:, hd * V_DIM:(hd + 1) * V_DIM].T
            q2_scr[hd, :, r * sub_w:(r + 1) * sub_w] = jnp.concatenate(
                [jnp.where(dim_row < HEAD_DIM, qt, 0.0), jnp.where(dim_row >= HEAD_DIM, qt, 0.0)],
                axis=1).astype(BF16)

    m_scr[...] = jnp.full_like(m_scr, -jnp.inf)
    acc_scr[...] = jnp.zeros_like(acc_scr)

    ones_rows = (lax.broadcasted_iota(jnp.int32, (ACC_ROWS - V_DIM, tk), 0) == 0).astype(BF16)

    chunks_per_lb = LANES // CHUNK
    lane_chunk = lax.broadcasted_iota(jnp.int32, (CHUNK, LANES), 1) // CHUNK

    def diag_pieces(s):
        pieces = {}
        for kc in range(tk // CHUNK):
            for lb in range(s.shape[1] // LANES):
                pc = s[kc * CHUNK:(kc + 1) * CHUNK, lb * LANES:(lb + 1) * LANES]
                if lb < sub_w // LANES:
                    first_qc = (lb % (tk // LANES)) * chunks_per_lb
                    if kc >= first_qc + chunks_per_lb:
                        continue
                    if kc > first_qc:
                        pc = jnp.where(lane_chunk + first_qc >= kc, pc, -jnp.inf)
                pieces[kc, lb] = pc
        return pieces

    def scores(hd, j, lo):
        k_blk = k_ref[0, hd, pl.ds(pl.multiple_of(j * tk, tk), tk), :]
        s_scr[hd % S_SLOTS, :, :width - lo] = jnp.dot(
            k_blk, q2_scr[hd, :, lo:], preferred_element_type=F32)

    def softmax_pv(hd, j, lo, diag):
        n_lb = (width - lo) // LANES
        s = s_scr[hd % S_SLOTS, :, :width - lo]
        m_prev = m_scr[hd, :, lo:]
        if diag:
            pieces = diag_pieces(s)
            m_blk = jnp.concatenate(
                [functools.reduce(jnp.maximum, [jnp.max(pc, axis=0, keepdims=True)
                                                for (_, lb_), pc in pieces.items() if lb_ == lb])
                 for lb in range(n_lb)], axis=1)
        else:
            m_blk = jnp.max(s, axis=0, keepdims=True)
        m_next = jnp.maximum(m_prev, m_blk)
        alpha = jnp.exp2(m_prev - m_next)
        if diag:
            p = jnp.concatenate(
                [jnp.concatenate(
                    [jnp.exp2(pieces[kc, lb] - m_next[:, lb * LANES:(lb + 1) * LANES]).astype(BF16)
                     if (kc, lb) in pieces else jnp.zeros((CHUNK, LANES), BF16)
                     for lb in range(n_lb)], axis=1)
                 for kc in range(tk // CHUNK)], axis=0)
        else:
            p = jnp.exp2(s - m_next).astype(BF16)
        m_scr[hd, :, lo:] = m_next
        vt_ext = jnp.concatenate([vt_ref[0, hd, j], ones_rows], axis=0)
        acc_scr[hd, :, lo:] = alpha * acc_scr[hd, :, lo:] + jnp.dot(
            vt_ext, p, preferred_element_type=F32)

    for hd in range(QK_AHEAD):
        scores(hd, 0, 0)

    full_steps = [(b, hd) for b in range(n_sub) for hd in range(n_heads)]

    def kv_body(jj, carry):
        for t, (b, hd) in enumerate(full_steps):
            b2, hd2 = divmod(t + QK_AHEAD, n_heads)
            scores(hd2, jj * n_sub + b2, 0)
            softmax_pv(hd, jj * n_sub + b, 0, diag=False)
        return carry

    lax.fori_loop(0, i, kv_body, 0)

    diag_steps = [(d, hd) for d in range(n_sub) for hd in range(n_heads)]
    for t, (d, hd) in enumerate(diag_steps):
        if t + QK_AHEAD < len(diag_steps):
            d2, hd2 = diag_steps[t + QK_AHEAD]
            scores(hd2, i * n_sub + d2, d2 * sub_w)
        softmax_pv(hd, i * n_sub + d, d * sub_w, diag=True)

    zs = [jnp.dot(hs[r], wz_ref[...], preferred_element_type=F32) for r in range(n_sub)]

    lv = lam_ref[...]
    lam = (jnp.exp(jnp.sum(lv[0:1] * lv[1:2], axis=-1, keepdims=True))
           - jnp.exp(jnp.sum(lv[2:3] * lv[3:4], axis=-1, keepdims=True)) + lam_init)
    for hd in range(n_heads):
        acc = acc_scr[hd]
        a = acc[:V_DIM] * (1.0 / acc[V_DIM:V_DIM + 1])
        for r in range(n_sub):
            lo = r * sub_w
            o_t = a[:, lo:lo + tk] - lam * a[:, lo + tk:lo + sub_w]
            o_t = o_t * lax.rsqrt(jnp.mean(o_t * o_t, axis=0, keepdims=True) + SUBLN_EPS)
            o_scr[rows(r), hd * V_DIM:(hd + 1) * V_DIM] = (
                o_t.T * subln_ref[...] * (1.0 - lam_init))

    for r in range(n_sub):
        y = jnp.dot((o_scr[rows(r), :] * jax.nn.silu(zs[r])).astype(BF16), wout_ref[...],
                    preferred_element_type=F32)
        o_ref[0, rows(r), :] = x_ref[0, rows(r), :] + gate_ref[0] * _rms(y, gpost_ref[...], EPS)


def _attn_layer(x, shift, scale, gate, gpre, gpost, w_q, w_z, lam_vec, subln, w_out, k, vt,
                lam_init):
    bsz, seq, d = x.shape
    n_heads = k.shape[1]
    tq, tk = ATTN_TILE, ATTN_BLOCK
    width = 2 * tq
    vec = pl.BlockSpec((1, 1, d), lambda b, s: (b, 0, 0))
    return pl.pallas_call(
        functools.partial(_attn_layer_kernel, tq=tq, tk=tk, lam_init=lam_init),
        grid=(bsz, seq // tq),
        in_specs=[
            pl.BlockSpec((1, tq, d), lambda b, s: (b, s, 0)),
            vec, vec, vec,
            _const_spec((1, d)), _const_spec((1, d)),
            _const_spec(w_q.shape), _const_spec(w_z.shape),
            _const_spec(lam_vec.shape), _const_spec((1, V_DIM)), _const_spec(w_out.shape),
            pl.BlockSpec((1,) + k.shape[1:], lambda b, s: (b, 0, 0, 0)),
            pl.BlockSpec((1,) + vt.shape[1:], lambda b, s: (b, 0, 0, 0, 0)),
        ],
        out_specs=pl.BlockSpec((1, tq, d), lambda b, s: (b, s, 0)),
        out_shape=jax.ShapeDtypeStruct(x.shape, F32),
        scratch_shapes=[
            pltpu.VMEM((n_heads, V_DIM, width), BF16),
            pltpu.VMEM((S_SLOTS, tk, width), F32),
            pltpu.VMEM((n_heads, 1, width), F32),
            pltpu.VMEM((n_heads, ACC_ROWS, width), F32),
            pltpu.VMEM((tq, n_heads * V_DIM), F32),
        ],
        compiler_params=pltpu.CompilerParams(
            dimension_semantics=("arbitrary", "arbitrary"),
            vmem_limit_bytes=VMEM_LIMIT_BYTES),
        name="attn_layer",
    )(x, shift, scale, gate, gpre, gpost, w_q, w_z, lam_vec, subln, w_out, k, vt)


def kernel(x, c, ada_w, ada_b, norm_pre, norm_post, a_w_in, a_w_group, a_scale, a_w_out,
           kv_norm, kv_ada_w, kv_ada_b, w_kv, b_w_in, b_lambda, b_subln, b_w_out):
    bsz, seq, d = x.shape
    depth = ada_w.shape[0]
    n_a = a_w_in.shape[0]
    qk_w = w_kv.shape[1] // 2
    n_heads = qk_w // V_DIM

    ada = _ada_proj(c, ada_w, ada_b[:, None, :], tn=d)
    kv_ada = _ada_proj(c, kv_ada_w[None], kv_ada_b[None, None, :], tn=d)[0]

    def vecs(a, n):
        return [a[:, None, j * d:(j + 1) * d] for j in range(n)]

    k = vt = None
    for l in range(depth):
        shift, scale, gate = vecs(ada[l], 3)
        gpre = norm_pre[l][None, :]
        gpost = norm_post[l][None, :]
        if l < n_a:
            x = _pool_layer(x, shift, scale, gate, gpre, gpost,
                            a_w_in[l].astype(BF16), a_w_group[l].astype(BF16),
                            a_scale[l][None, :], a_w_out[l].astype(BF16))
        else:
            if l == n_a:
                kv_shift, kv_scale = vecs(kv_ada, 2)
                w_kv16 = w_kv.astype(BF16)
                k, vt = _shared_kv(x, kv_shift, kv_scale, kv_norm[None, :],
                                   w_kv16[:, :qk_w], w_kv16[:, qk_w:].T, n_heads)
            j = l - n_a
            w_in = b_w_in[j].astype(BF16)
            x = _attn_layer(x, shift, scale, gate, gpre, gpost,
                            w_in[:, :qk_w], w_in[:, qk_w:], b_lambda[j], b_subln[j][None, :],
                            b_w_out[j].astype(BF16), k, vt, _lambda_init(l))
    return x
```

```python
import functools
import math

import jax
import jax.numpy as jnp
from jax import lax
from jax.experimental import pallas as pl
from jax.experimental.pallas import tpu as pltpu

F32 = jnp.float32
BF16 = jnp.bfloat16

CHUNK = 64
POOL_WINDOWS = (2, 4, 8, 16)
HEAD_DIM = 64
V_DIM = 2 * HEAD_DIM
EPS = 1e-6
SUBLN_EPS = 1e-5
LOG2E = math.log2(math.e)

LANES = 128
BF16_ROWS = 16
ACC_ROWS = V_DIM + BF16_ROWS

POOL_HALO = 16
POOL_TILE = 512
POOL_SUBTILE = 256
KV_TILE = 512
ATTN_TILE = 512
ATTN_BLOCK = 256
SOFTMAX_LANES = 512
QK_AHEAD = 2
S_SLOTS = 4
VMEM_LIMIT_BYTES = 56 * 1024 * 1024


def _lambda_init(layer_idx):
    return 0.8 - 0.6 * math.exp(-0.3 * layer_idx)


def _rms(x, g, eps):
    return x * lax.rsqrt(jnp.mean(x * x, axis=-1, keepdims=True) + eps) * g


def _silu(z):
    hz = 0.5 * z
    return hz + hz * jnp.tanh(hz)


def _const_spec(shape):
    zeros = (0,) * len(shape)
    return pl.BlockSpec(shape, lambda *_: zeros, pipeline_mode=pl.Buffered(1))


_NT_DIMS = (((1,), (1,)), ((), ()))


def _ada_kernel(c_ref, w_ref, b_ref, o_ref):
    cond = jax.nn.silu(c_ref[...]).astype(BF16)
    w = w_ref[0].astype(BF16)
    o_ref[0] = jnp.dot(cond, w, preferred_element_type=F32) + b_ref[0]


def _ada_proj(c, w, b, tn):
    n_layers, d, n = w.shape
    bsz = c.shape[0]
    return pl.pallas_call(
        _ada_kernel,
        grid=(n_layers, n // tn),
        in_specs=[
            pl.BlockSpec((bsz, d), lambda l, j: (0, 0)),
            pl.BlockSpec((1, d, tn), lambda l, j: (l, 0, j)),
            pl.BlockSpec((1, 1, tn), lambda l, j: (l, 0, j)),
        ],
        out_specs=pl.BlockSpec((1, bsz, tn), lambda l, j: (l, 0, j)),
        out_shape=jax.ShapeDtypeStruct((n_layers, bsz, n), F32),
        compiler_params=pltpu.CompilerParams(
            dimension_semantics=("arbitrary", "arbitrary"),
            vmem_limit_bytes=VMEM_LIMIT_BYTES),
        name="ada_proj",
    )(c, w, b)


def _pool_layer_kernel(x_ref, shift_ref, scale_ref, gate_ref, gpre_ref, gpost_ref,
                       win_ref, wg_ref, chs_ref, wout_ref, o_ref, carry_ref, *, tm, sub):
    s_idx = pl.program_id(1)
    e_a = wout_ref.shape[0]
    g_a = e_a // len(POOL_WINDOWS)
    n_sub = tm // sub

    @pl.when(s_idx == 0)
    def _():
        carry_ref[...] = jnp.zeros_like(carry_ref)

    gain = gpre_ref[...] * (1.0 + scale_ref[0])
    xs, us, zs = [], [], []
    for r in range(n_sub):
        x = x_ref[0, r * sub:(r + 1) * sub, :]
        h = x * lax.rsqrt(jnp.mean(x * x, axis=-1, keepdims=True) + EPS) * gain + shift_ref[0]
        uz = jnp.dot(h.astype(BF16), win_ref[...], preferred_element_type=F32)
        xs.append(x)
        us.append(uz[:, :e_a])
        zs.append(uz[:, e_a:])

    acts = []
    for r in range(n_sub):
        u, z = us[r], zs[r]
        prev = carry_ref[...] if r == 0 else us[r - 1][sub - POOL_HALO:, :]
        ext = jnp.concatenate([prev, u], axis=0)
        t1 = (s_idx * tm + r * sub + 1
              + lax.broadcasted_iota(jnp.int32, (sub, 1), 0)).astype(F32)
        parts = []
        for g, w in enumerate(POOL_WINDOWS):
            cols = slice(g * g_a, (g + 1) * g_a)
            win = ext[:, cols]
            shift = 1
            while shift < w:
                win = win + pltpu.roll(win, shift, axis=0)
                shift *= 2
            inv_cnt = 1.0 / jnp.minimum(t1, float(w))
            pooled = win[POOL_HALO:, :] * inv_cnt - u[:, cols]
            mixed = jnp.dot(pooled.astype(BF16), wg_ref[g], preferred_element_type=F32)
            parts.append((mixed * chs_ref[:, cols] * _silu(z[:, cols])).astype(BF16))
        acts.append(jnp.concatenate(parts, axis=-1))
    carry_ref[...] = us[-1][sub - POOL_HALO:, :]

    for r in range(n_sub):
        y = jnp.dot(acts[r], wout_ref[...], preferred_element_type=F32)
        o_ref[0, r * sub:(r + 1) * sub, :] = xs[r] + gate_ref[0] * _rms(y, gpost_ref[...], EPS)


def _pool_layer(x, shift, scale, gate, gpre, gpost, w_in, w_group, ch_scale, w_out):
    bsz, seq, d = x.shape
    e_a = w_out.shape[0]
    tm = POOL_TILE
    vec = pl.BlockSpec((1, 1, d), lambda b, s: (b, 0, 0))
    return pl.pallas_call(
        functools.partial(_pool_layer_kernel, tm=tm, sub=POOL_SUBTILE),
        grid=(bsz, seq // tm),
        in_specs=[
            pl.BlockSpec((1, tm, d), lambda b, s: (b, s, 0)),
            vec, vec, vec,
            _const_spec((1, d)), _const_spec((1, d)),
            _const_spec(w_in.shape), _const_spec(w_group.shape),
            _const_spec((1, e_a)), _const_spec(w_out.shape),
        ],
        out_specs=pl.BlockSpec((1, tm, d), lambda b, s: (b, s, 0)),
        out_shape=jax.ShapeDtypeStruct(x.shape, F32),
        scratch_shapes=[pltpu.VMEM((POOL_HALO, e_a), F32)],
        compiler_params=pltpu.CompilerParams(
            dimension_semantics=("arbitrary", "arbitrary"),
            vmem_limit_bytes=VMEM_LIMIT_BYTES),
        name="pool_layer",
    )(x, shift, scale, gate, gpre, gpost, w_in, w_group, ch_scale, w_out)


def _kv_kernel(x_ref, shift_ref, scale_ref, g_ref, wk_ref, wvt_ref, k_ref, vt_ref):
    n_heads = k_ref.shape[1]
    n_blk, tk = vt_ref.shape[2], vt_ref.shape[4]
    x = x_ref[0]
    hk = (_rms(x, g_ref[...], EPS) * (1.0 + scale_ref[0]) + shift_ref[0]).astype(BF16)
    k = jnp.dot(hk, wk_ref[...], preferred_element_type=F32)
    vt = lax.dot_general(wvt_ref[...], hk, _NT_DIMS, preferred_element_type=F32)
    for hd in range(n_heads):
        k_ref[0, hd] = k[:, hd * V_DIM:(hd + 1) * V_DIM].astype(BF16)
        for jb in range(n_blk):
            vt_ref[0, hd, jb] = vt[hd * V_DIM:(hd + 1) * V_DIM,
                                   jb * tk:(jb + 1) * tk].astype(BF16)


def _shared_kv(x, shift, scale, g, w_k, w_vt, n_heads):
    bsz, seq, d = x.shape
    tm, tk = KV_TILE, ATTN_BLOCK
    vec = pl.BlockSpec((1, 1, d), lambda b, s: (b, 0, 0))
    return pl.pallas_call(
        _kv_kernel,
        grid=(bsz, seq // tm),
        in_specs=[
            pl.BlockSpec((1, tm, d), lambda b, s: (b, s, 0)),
            vec, vec, _const_spec((1, d)), _const_spec(w_k.shape), _const_spec(w_vt.shape),
        ],
        out_specs=(
            pl.BlockSpec((1, n_heads, tm, V_DIM), lambda b, s: (b, 0, s, 0)),
            pl.BlockSpec((1, n_heads, tm // tk, V_DIM, tk), lambda b, s: (b, 0, s, 0, 0)),
        ),
        out_shape=(
            jax.ShapeDtypeStruct((bsz, n_heads, seq, V_DIM), BF16),
            jax.ShapeDtypeStruct((bsz, n_heads, seq // tk, V_DIM, tk), BF16),
        ),
        compiler_params=pltpu.CompilerParams(
            dimension_semantics=("arbitrary", "arbitrary"),
            vmem_limit_bytes=VMEM_LIMIT_BYTES),
        name="shared_kv",
    )(x, shift, scale, g, w_k, w_vt)


def _attn_layer_kernel(x_ref, shift_ref, scale_ref, gate_ref, gpre_ref, gpost_ref,
                       wq_ref, wz_ref, lam_ref, subln_ref, wout_ref, k_ref, vt_ref,
                       o_ref, q2_scr, s_scr, m_scr, acc_scr, o_scr, *, tq, tk, lam_init):
    i = pl.program_id(1)
    n_heads = k_ref.shape[1]
    n_sub = tq // tk
    sub_w = 2 * tk
    width = n_sub * sub_w

    def rows(r):
        return slice(r * tk, (r + 1) * tk)

    gain = gpre_ref[...] * (1.0 + scale_ref[0])
    hs, qs = [], []
    for r in range(n_sub):
        x = x_ref[0, rows(r), :]
        hs.append((x * lax.rsqrt(jnp.mean(x * x, axis=-1, keepdims=True) + EPS) * gain
                   + shift_ref[0]).astype(BF16))
        qs.append(jnp.dot(hs[r], wq_ref[...], preferred_element_type=F32)
                  * (HEAD_DIM ** -0.5 * LOG2E))
    dim_row = lax.broadcasted_iota(jnp.int32, (V_DIM, tk), 0)
    for r in range(n_sub):
        q = qs[r]
        for hd in range(n_heads):
            qt = q[:, hd * V_DIM:(hd + 1) * V_DIM].T
            q2_scr[hd, :, r * sub_w:(r + 1) * sub_w] = jnp.concatenate(
                [jnp.where(dim_row < HEAD_DIM, qt, 0.0), jnp.where(dim_row >= HEAD_DIM, qt, 0.0)],
                axis=1).astype(BF16)

    m_scr[...] = jnp.full_like(m_scr, -jnp.inf)
    acc_scr[...] = jnp.zeros_like(acc_scr)

    ones_rows = (lax.broadcasted_iota(jnp.int32, (ACC_ROWS - V_DIM, tk), 0) == 0).astype(BF16)

    chunks_per_lb = LANES // CHUNK
    lane_chunk = lax.broadcasted_iota(jnp.int32, (CHUNK, LANES), 1) // CHUNK

    def diag_pieces(s):
        pieces = {}
        for kc in range(tk // CHUNK):
            for lb in range(s.shape[1] // LANES):
                pc = s[kc * CHUNK:(kc + 1) * CHUNK, lb * LANES:(lb + 1) * LANES]
                if lb < sub_w // LANES:
                    first_qc = (lb % (tk // LANES)) * chunks_per_lb
                    if kc >= first_qc + chunks_per_lb:
                        continue
                    if kc > first_qc:
                        pc = jnp.where(lane_chunk + first_qc >= kc, pc, -jnp.inf)
                pieces[kc, lb] = pc
        return pieces

    def scores(hd, j, lo):
        k_blk = k_ref[0, hd, pl.ds(pl.multiple_of(j * tk, tk), tk), :]
        for r in range(lo // sub_w, n_sub):
            s_scr[hd % S_SLOTS, :, r * sub_w - lo:(r + 1) * sub_w - lo] = jnp.dot(
                k_blk, q2_scr[hd, :, r * sub_w:(r + 1) * sub_w], preferred_element_type=F32)

    def softmax_pv(hd, j, lo, diag):
        vt_ext = jnp.concatenate([vt_ref[0, hd, j], ones_rows], axis=0)
        spans = []
        for r in range(lo // sub_w, n_sub):
            if diag and r * sub_w == lo:
                spans.append((r * sub_w, sub_w, True))
            else:
                spans += [(r * sub_w + c * SOFTMAX_LANES, SOFTMAX_LANES, False)
                          for c in range(sub_w // SOFTMAX_LANES)]
        for start, span_w, masked in spans:
            lanes = slice(start, start + span_w)
            s = s_scr[hd % S_SLOTS, :, start - lo:start - lo + span_w]
            m_prev = m_scr[hd, :, lanes]
            if masked:
                pieces = diag_pieces(s)
                m_blk = jnp.concatenate(
                    [functools.reduce(jnp.maximum, [jnp.max(pc, axis=0, keepdims=True)
                                                    for (_, lb_), pc in pieces.items() if lb_ == lb])
                     for lb in range(sub_w // LANES)], axis=1)
            else:
                m_blk = jnp.max(s, axis=0, keepdims=True)
            m_next = jnp.maximum(m_prev, m_blk)
            alpha = jnp.exp2(m_prev - m_next)
            if masked:
                p = jnp.concatenate(
                    [jnp.concatenate(
                        [jnp.exp2(pieces[kc, lb]
                                  - m_next[:, lb * LANES:(lb + 1) * LANES]).astype(BF16)
                         if (kc, lb) in pieces else jnp.zeros((CHUNK, LANES), BF16)
                         for lb in range(sub_w // LANES)], axis=1)
                     for kc in range(tk // CHUNK)], axis=0)
            else:
                p = jnp.exp2(s - m_next).astype(BF16)
            m_scr[hd, :, lanes] = m_next
            acc_scr[hd, :, lanes] = alpha * acc_scr[hd, :, lanes] + jnp.dot(
                vt_ext, p, preferred_element_type=F32)

    for hd in range(QK_AHEAD):
        scores(hd, 0, 0)

    full_steps = [(b, hd) for b in range(n_sub) for hd in range(n_heads)]

    def kv_body(jj, carry):
        for t, (b, hd) in enumerate(full_steps):
            b2, hd2 = divmod(t + QK_AHEAD, n_heads)
            scores(hd2, jj * n_sub + b2, 0)
            softmax_pv(hd, jj * n_sub + b, 0, diag=False)
        return carry

    lax.fori_loop(0, i, kv_body, 0)

    diag_steps = [(d, hd) for d in range(n_sub) for hd in range(n_heads)]
    for t, (d, hd) in enumerate(diag_steps):
        if t + QK_AHEAD < len(diag_steps):
            d2, hd2 = diag_steps[t + QK_AHEAD]
            scores(hd2, i * n_sub + d2, d2 * sub_w)
        softmax_pv(hd, i * n_sub + d, d * sub_w, diag=True)

    zs = [jnp.dot(hs[r], wz_ref[...], preferred_element_type=F32) for r in range(n_sub)]

    lv = lam_ref[...]
    lam = (jnp.exp(jnp.sum(lv[0:1] * lv[1:2], axis=-1, keepdims=True))
           - jnp.exp(jnp.sum(lv[2:3] * lv[3:4], axis=-1, keepdims=True)) + lam_init)
    for hd in range(n_heads):
        acc = acc_scr[hd]
        a = acc[:V_DIM] * (1.0 / acc[V_DIM:V_DIM + 1])
        for r in range(n_sub):
            lo = r * sub_w
            o_t = a[:, lo:lo + tk] - lam * a[:, lo + tk:lo + sub_w]
            o_t = o_t * lax.rsqrt(jnp.mean(o_t * o_t, axis=0, keepdims=True) + SUBLN_EPS)
            o_scr[rows(r), hd * V_DIM:(hd + 1) * V_DIM] = (
                o_t.T * subln_ref[...] * (1.0 - lam_init))

    for r in range(n_sub):
        y = jnp.dot((o_scr[rows(r), :] * _silu(zs[r])).astype(BF16), wout_ref[...],
                    preferred_element_type=F32)
        o_ref[0, rows(r), :] = x_ref[0, rows(r), :] + gate_ref[0] * _rms(y, gpost_ref[...], EPS)


def _attn_layer(x, shift, scale, gate, gpre, gpost, w_q, w_z, lam_vec, subln, w_out, k, vt,
                lam_init):
    bsz, seq, d = x.shape
    n_heads = k.shape[1]
    tq, tk = ATTN_TILE, ATTN_BLOCK
    width = 2 * tq
    vec = pl.BlockSpec((1, 1, d), lambda b, s: (b, 0, 0))
    return pl.pallas_call(
        functools.partial(_attn_layer_kernel, tq=tq, tk=tk, lam_init=lam_init),
        grid=(bsz, seq // tq),
        in_specs=[
            pl.BlockSpec((1, tq, d), lambda b, s: (b, s, 0)),
            vec, vec, vec,
            _const_spec((1, d)), _const_spec((1, d)),
            _const_spec(w_q.shape), _const_spec(w_z.shape),
            _const_spec(lam_vec.shape), _const_spec((1, V_DIM)), _const_spec(w_out.shape),
            pl.BlockSpec((1,) + k.shape[1:], lambda b, s: (b, 0, 0, 0)),
            pl.BlockSpec((1,) + vt.shape[1:], lambda b, s: (b, 0, 0, 0, 0)),
        ],
        out_specs=pl.BlockSpec((1, tq, d), lambda b, s: (b, s, 0)),
        out_shape=jax.ShapeDtypeStruct(x.shape, F32),
        scratch_shapes=[
            pltpu.VMEM((n_heads, V_DIM, width), BF16),
            pltpu.VMEM((S_SLOTS, tk, width), F32),
            pltpu.VMEM((n_heads, 1, width), F32),
            pltpu.VMEM((n_heads, ACC_ROWS, width), F32),
            pltpu.VMEM((tq, n_heads * V_DIM), F32),
        ],
        compiler_params=pltpu.CompilerParams(
            dimension_semantics=("arbitrary", "arbitrary"),
            vmem_limit_bytes=VMEM_LIMIT_BYTES),
        name="attn_layer",
    )(x, shift, scale, gate, gpre, gpost, w_q, w_z, lam_vec, subln, w_out, k, vt)


def kernel(x, c, ada_w, ada_b, norm_pre, norm_post, a_w_in, a_w_group, a_scale, a_w_out,
           kv_norm, kv_ada_w, kv_ada_b, w_kv, b_w_in, b_lambda, b_subln, b_w_out):
    bsz, seq, d = x.shape
    depth = ada_w.shape[0]
    n_a = a_w_in.shape[0]
    qk_w = w_kv.shape[1] // 2
    n_heads = qk_w // V_DIM

    ada = _ada_proj(c, ada_w, ada_b[:, None, :], tn=d)
    kv_ada = _ada_proj(c, kv_ada_w[None], kv_ada_b[None, None, :], tn=d)[0]

    def vecs(a, n):
        return [a[:, None, j * d:(j + 1) * d] for j in range(n)]

    k = vt = None
    for l in range(depth):
        shift, scale, gate = vecs(ada[l], 3)
        gpre = norm_pre[l][None, :]
        gpost = norm_post[l][None, :]
        if l < n_a:
            x = _pool_layer(x, shift, scale, gate, gpre, gpost,
                            a_w_in[l].astype(BF16), a_w_group[l].astype(BF16),
                            a_scale[l][None, :], a_w_out[l].astype(BF16))
        else:
            if l == n_a:
                kv_shift, kv_scale = vecs(kv_ada, 2)
                w_kv16 = w_kv.astype(BF16)
                k, vt = _shared_kv(x, kv_shift, kv_scale, kv_norm[None, :],
                                   w_kv16[:, :qk_w], w_kv16[:, qk_w:].T, n_heads)
            j = l - n_a
            w_in = b_w_in[j].astype(BF16)
            x = _attn_layer(x, shift, scale, gate, gpre, gpost,
                            w_in[:, :qk_w], w_in[:, qk_w:], b_lambda[j], b_subln[j][None, :],
                            b_w_out[j].astype(BF16), k, vt, _lambda_init(l))
    return x
```
